```python
import jax
import jax.numpy as jnp
from jax import lax
import numpy as np

D_MODEL = 1024
BATCH = 16
SEQ = 2048
DEPTH = 4

MIX_WIDTH = D_MODEL
MOBA_HEADS = 4
MOBA_WIDTH = MIX_WIDTH // 4
MOBA_DH = MOBA_WIDTH // MOBA_HEADS
MOBA_BLOCK = 256
MOBA_TOPK = 3
MOBA_Q_CHUNK = 16
MLSTM_HEADS = 4
MLSTM_WIDTH = MIX_WIDTH // 2
MLSTM_DV = MLSTM_WIDTH // MLSTM_HEADS
MLSTM_DQK = MLSTM_DV // 2
MLSTM_CHUNK = 64
MLSTM_CONV = 4
NSA_HEADS = 4
NSA_WIDTH = MIX_WIDTH - MOBA_WIDTH - MLSTM_WIDTH
NSA_DH = NSA_WIDTH // NSA_HEADS
NSA_CMP_LEN = 32
NSA_CMP_STRIDE = 16
NSA_CMP_HIDDEN = 256
NSA_SEL_BLOCK = 64
NSA_TOPN = 8
NSA_WINDOW = 512
NSA_Q_CHUNK = 64
D_FF = 4 * D_MODEL
NORM_EPS = 1e-6
NEG_INF = -1e30

IN_SPLITS = (
    MOBA_WIDTH, MOBA_WIDTH, MOBA_WIDTH,
    MLSTM_HEADS * MLSTM_DQK, MLSTM_HEADS * MLSTM_DQK,
    MLSTM_WIDTH, MLSTM_WIDTH,
    MLSTM_HEADS, MLSTM_HEADS,
    NSA_WIDTH,
    NSA_DH, NSA_DH, NSA_DH, NSA_DH, NSA_DH, NSA_DH,
    3 * NSA_HEADS,
)
D_IN = sum(IN_SPLITS)

kernel_name = 'hybrid_moba_mlstm_nsa_trunk'


def _rms_norm(x, g):
    xf = x.astype(jnp.float32)
    y = xf * lax.rsqrt(jnp.mean(xf * xf, axis=-1, keepdims=True) + NORM_EPS)
    return (y * g.astype(jnp.float32)).astype(x.dtype)


def _head_norm(y, g):
    b, s, h, dh = y.shape
    return _rms_norm(y, g.reshape(h, dh)).reshape(b, s, h * dh)


def _causal_conv(x, w, bias):
    k, c = w.shape
    y = lax.conv_general_dilated(x, w[:, None, :].astype(x.dtype), window_strides=(1,),
                                 padding=[(k - 1, 0)], dimension_numbers=('NWC', 'WIO', 'NWC'),
                                 feature_group_count=c)
    return y + bias.astype(x.dtype)


def _moba_attention(q, k, v):
    b, h, s, dh = q.shape
    n_blk = -(-s // MOBA_BLOCK)
    s_pad = n_blk * MOBA_BLOCK
    pad = ((0, 0), (0, 0), (0, s_pad - s), (0, 0))
    q, k, v = jnp.pad(q, pad), jnp.pad(k, pad), jnp.pad(v, pad)
    k_blk = k.reshape(b * h, n_blk, MOBA_BLOCK, dh)
    v_blk = v.reshape(b * h, n_blk, MOBA_BLOCK, dh)
    k_mean = jnp.mean(k_blk.astype(jnp.float32), axis=2).reshape(b, h, n_blk, dh)
    n_top = min(MOBA_TOPK, n_blk - 1)
    cq = MOBA_Q_CHUNK
    n_chunks = s_pad // cq
    q_chunks = jnp.moveaxis(q.reshape(b, h, n_chunks, cq, dh), 2, 0)
    scale = dh ** -0.5
    bh_ids = jnp.arange(b * h)[:, None]
    blk_ids = jnp.arange(n_blk)

    def one_chunk(args):
        c, qb = args
        q0 = c * cq
        q_pos = q0 + jnp.arange(cq)
        cur = q0 // MOBA_BLOCK
        k_own = lax.dynamic_slice_in_dim(k, cur * MOBA_BLOCK, MOBA_BLOCK, axis=2)
        v_own = lax.dynamic_slice_in_dim(v, cur * MOBA_BLOCK, MOBA_BLOCK, axis=2)
        k_pos = cur * MOBA_BLOCK + jnp.arange(MOBA_BLOCK)
        s_own = jnp.einsum('bhqd,bhkd->bhqk', qb, k_own).astype(jnp.float32) * scale
        s_own = jnp.where(k_pos[None, :] <= q_pos[:, None], s_own, NEG_INF)
        if n_top == 0:
            p_own = jax.nn.softmax(s_own, axis=-1).astype(v.dtype)
            return jnp.einsum('bhqk,bhkd->bhqd', p_own, v_own)
        gate = jnp.einsum('bhqd,bhnd->bhqn', qb.astype(jnp.float32), k_mean)
        gate = jnp.where(blk_ids < cur, gate, -jnp.inf)
        g_val, g_idx = lax.top_k(gate, n_top)
        flat_idx = g_idx.reshape(b * h, cq * n_top)
        k_sel = k_blk[bh_ids, flat_idx].reshape(b, h, cq, n_top, MOBA_BLOCK, dh)
        v_sel = v_blk[bh_ids, flat_idx].reshape(b, h, cq, n_top, MOBA_BLOCK, dh)
        s_sel = jnp.einsum('bhqd,bhqjkd->bhqjk', qb, k_sel).astype(jnp.float32) * scale
        s_sel = jnp.where((g_val > -jnp.inf)[..., None], s_sel, NEG_INF)
        scores = jnp.concatenate([s_sel.reshape(b, h, cq, n_top * MOBA_BLOCK), s_own], axis=-1)
        p = jax.nn.softmax(scores, axis=-1).astype(v.dtype)
        p_sel = p[..., : n_top * MOBA_BLOCK].reshape(b, h, cq, n_top, MOBA_BLOCK)
        p_own = p[..., n_top * MOBA_BLOCK:]
        return (jnp.einsum('bhqjk,bhqjkd->bhqd', p_sel, v_sel)
                + jnp.einsum('bhqk,bhkd->bhqd', p_own, v_own))

    out = lax.map(one_chunk, (jnp.arange(n_chunks), q_chunks))
    return jnp.moveaxis(out, 0, 2).reshape(b, h, s_pad, dh)[:, :, :s]


def _mlstm_chunkwise(q, k, v, i_pre, f_pre):
    b, s, h, dk = q.shape
    dv = v.shape[-1]
    L = MLSTM_CHUNK
    nc = s // L
    f32 = jnp.float32

    def chunks(t):
        t = jnp.moveaxis(t.astype(f32), 2, 1)
        return t.reshape((b, h, nc, L) + t.shape[3:])

    q = chunks(q)
    k = chunks(k) * (dk ** -0.5)
    v = chunks(v)
    ig = chunks(i_pre)
    logf = jax.nn.log_sigmoid(chunks(f_pre))
    bcum = jnp.cumsum(logf, axis=-1)
    b_end = bcum[..., -1]
    w_end = b_end[..., None] - bcum + ig
    a_end = jnp.max(w_end, axis=-1)
    e_end = jnp.exp(w_end - a_end[..., None])
    c_chunk = jnp.einsum('bhcs,bhcsv,bhcsk->bhcvk', e_end, v, k)
    n_chunk = jnp.einsum('bhcs,bhcsk->bhck', e_end, k)

    def step(carry, xs):
        c_st, n_st, m_st = carry
        be, ac, cc, nn = xs
        m_new = jnp.maximum(be + m_st, ac)
        decay = jnp.exp(be + m_st - m_new)
        inj = jnp.exp(ac - m_new)
        c_new = decay[..., None, None] * c_st + inj[..., None, None] * cc
        n_new = decay[..., None] * n_st + inj[..., None] * nn
        return (c_new, n_new, m_new), (c_st, n_st, m_st)

    init = (jnp.zeros((b, h, dv, dk), f32), jnp.zeros((b, h, dk), f32), jnp.zeros((b, h), f32))
    xs = (jnp.moveaxis(b_end, 2, 0), jnp.moveaxis(a_end, 2, 0),
          jnp.moveaxis(c_chunk, 2, 0), jnp.moveaxis(n_chunk, 2, 0))
    _, (c_prev, n_prev, m_prev) = lax.scan(step, init, xs)
    c_prev = jnp.moveaxis(c_prev, 0, 2)
    n_prev = jnp.moveaxis(n_prev, 0, 2)
    m_prev = jnp.moveaxis(m_prev, 0, 2)

    causal = jnp.tril(jnp.ones((L, L), dtype=bool))
    d_log = jnp.where(causal, bcum[..., :, None] - bcum[..., None, :] + ig[..., None, :], -jnp.inf)
    m_inter = bcum + m_prev[..., None]
    m_t = jnp.maximum(m_inter, jnp.max(d_log, axis=-1))
    s_intra = jnp.einsum('bhctd,bhcsd->bhcts', q, k) * jnp.exp(d_log - m_t[..., None])
    w_inter = jnp.exp(m_inter - m_t)
    num = (jnp.einsum('bhcts,bhcsv->bhctv', s_intra, v)
           + jnp.einsum('bhctd,bhcvd->bhctv', q, c_prev) * w_inter[..., None])
    den = s_intra.sum(axis=-1) + jnp.einsum('bhctd,bhcd->bhct', q, n_prev) * w_inter
    hid = num / jnp.maximum(jnp.abs(den), jnp.exp(-m_t))[..., None]
    return jnp.moveaxis(hid.reshape(b, h, s, dv), 1, 2)


def _nsa_attention(q, kc, vc, ksl, vsl, kwn, vwn, gate_pre, cmp_pe, cmp_w1, cmp_w2):
    b, h, s, dh = q.shape
    f32 = jnp.float32
    scale = dh ** -0.5
    t_pos = jnp.arange(s)

    n_cmp = (s - NSA_CMP_LEN) // NSA_CMP_STRIDE + 1
    cmp_idx = jnp.arange(n_cmp)[:, None] * NSA_CMP_STRIDE + jnp.arange(NSA_CMP_LEN)[None, :]

    def compress(t, j):
        blk = t[:, cmp_idx] + cmp_pe[j]
        blk = blk.reshape(b, n_cmp, NSA_CMP_LEN * dh)
        return jax.nn.gelu(blk @ cmp_w1[j]) @ cmp_w2[j]

    k_cmp = compress(kc, 0)
    v_cmp = compress(vc, 1)
    cmp_end = jnp.arange(n_cmp) * NSA_CMP_STRIDE + NSA_CMP_LEN - 1
    vis = cmp_end[None, :] <= t_pos[:, None]
    s_cmp = jnp.einsum('bhtd,bnd->bhtn', q, k_cmp).astype(f32) * scale
    p_cmp = jnp.where(vis, jax.nn.softmax(jnp.where(vis, s_cmp, NEG_INF), axis=-1), 0.0)
    o_cmp = jnp.einsum('bhtn,bnd->bhtd', p_cmp.astype(v_cmp.dtype), v_cmp)

    n_sel = s // NSA_SEL_BLOCK
    lo = np.arange(n_cmp)[:, None] * NSA_CMP_STRIDE
    sl = np.arange(n_sel)[None, :] * NSA_SEL_BLOCK
    overlap = np.maximum(np.minimum(lo + NSA_CMP_LEN, sl + NSA_SEL_BLOCK) - np.maximum(lo, sl), 0) / NSA_CMP_LEN
    p_slc = jnp.einsum('btn,nj->btj', p_cmp.sum(axis=1), jnp.asarray(overlap, dtype=f32))
    blk_ids = jnp.arange(n_sel)[None, :]
    cur = (t_pos // NSA_SEL_BLOCK)[:, None]
    forced = (blk_ids == 0) | (blk_ids == cur) | (blk_ids == cur - 1)
    score = jnp.where(blk_ids > cur, -jnp.inf, jnp.where(forced, jnp.inf, p_slc))
    top = min(NSA_TOPN, n_sel)
    s_val, s_idx = lax.top_k(score, top)
    s_valid = s_val > -jnp.inf

    cq = NSA_Q_CHUNK
    n_ch = s // cq
    k_sel_blk = ksl.reshape(b, n_sel, NSA_SEL_BLOCK, dh)
    v_sel_blk = vsl.reshape(b, n_sel, NSA_SEL_BLOCK, dh)
    win_pad = ((0, 0), (NSA_WINDOW, 0), (0, 0))
    k_win_pad = jnp.pad(kwn, win_pad)
    v_win_pad = jnp.pad(vwn, win_pad)
    b_ids = jnp.arange(b)[:, None]
    q_c = jnp.moveaxis(q.reshape(b, h, n_ch, cq, dh), 2, 0)
    idx_c = jnp.moveaxis(s_idx.reshape(b, n_ch, cq, top), 1, 0)
    val_c = jnp.moveaxis(s_valid.reshape(b, n_ch, cq, top), 1, 0)

    def one_chunk(args):
        c, qb, idx, valid = args
        q0 = c * cq
        q_pos = q0 + jnp.arange(cq)
        flat_idx = idx.reshape(b, cq * top)
        k_sel = k_sel_blk[b_ids, flat_idx].reshape(b, cq, top, NSA_SEL_BLOCK, dh)
        v_sel = v_sel_blk[b_ids, flat_idx].reshape(b, cq, top, NSA_SEL_BLOCK, dh)
        k_pos = idx[..., None] * NSA_SEL_BLOCK + jnp.arange(NSA_SEL_BLOCK)
        m_sel = valid[..., None] & (k_pos <= q_pos[None, :, None, None])
        s_sel = jnp.einsum('bhqd,bqjkd->bhqjk', qb, k_sel).astype(f32) * scale
        s_sel = jnp.where(m_sel[:, None], s_sel, NEG_INF).reshape(b, h, cq, top * NSA_SEL_BLOCK)
        p_sel = jax.nn.softmax(s_sel, axis=-1).astype(v_sel.dtype).reshape(b, h, cq, top, NSA_SEL_BLOCK)
        o_sel = jnp.einsum('bhqjk,bqjkd->bhqd', p_sel, v_sel)
        k_win = lax.dynamic_slice_in_dim(k_win_pad, q0, NSA_WINDOW + cq, axis=1)
        v_win = lax.dynamic_slice_in_dim(v_win_pad, q0, NSA_WINDOW + cq, axis=1)
        w_pos = q0 - NSA_WINDOW + jnp.arange(NSA_WINDOW + cq)
        rel = q_pos[:, None] - w_pos[None, :]
        m_win = (w_pos[None, :] >= 0) & (rel >= 0) & (rel < NSA_WINDOW)
        s_win = jnp.einsum('bhqd,bkd->bhqk', qb, k_win).astype(f32) * scale
        p_win = jax.nn.softmax(jnp.where(m_win, s_win, NEG_INF), axis=-1).astype(v_win.dtype)
        o_win = jnp.einsum('bhqk,bkd->bhqd', p_win, v_win)
        return o_sel, o_win

    o_sel, o_win = lax.map(one_chunk, (jnp.arange(n_ch), q_c, idx_c, val_c))
    o_sel = jnp.moveaxis(o_sel, 0, 2).reshape(b, h, s, dh)
    o_win = jnp.moveaxis(o_win, 0, 2).reshape(b, h, s, dh)
    g = jax.nn.sigmoid(gate_pre.astype(f32)).transpose(0, 2, 1, 3)
    out = g[..., 0:1] * o_cmp + g[..., 1:2] * o_sel + g[..., 2:3] * o_win
    return out.astype(q.dtype).transpose(0, 2, 1, 3)


def _hybrid_layer(x, norm_g, w_in, conv_w, conv_b, gate_b, cmp_pe, cmp_w1, cmp_w2, head_g, w_out, w1, w2):
    b, s, _ = x.shape
    hin = _rms_norm(x, norm_g[0])
    z = hin @ w_in
    split_points = [int(p) for p in np.cumsum(IN_SPLITS)[:-1]]
    (mq, mk, mv, lq, lk, lv, lo, li, lf,
     nq, kc, vc, ksl, vsl, kwn, vwn, ng) = jnp.split(z, split_points, axis=-1)

    def heads(t, n):
        return t.reshape(b, s, n, -1).transpose(0, 2, 1, 3)

    y_moba = _moba_attention(heads(mq, MOBA_HEADS), heads(mk, MOBA_HEADS), heads(mv, MOBA_HEADS))
    y_moba = y_moba.transpose(0, 2, 1, 3)
    qk = jax.nn.silu(_causal_conv(jnp.concatenate([lq, lk], axis=-1), conv_w, conv_b))
    lq, lk = jnp.split(qk, 2, axis=-1)
    y_ml = _mlstm_chunkwise(lq.reshape(b, s, MLSTM_HEADS, MLSTM_DQK), lk.reshape(b, s, MLSTM_HEADS, MLSTM_DQK),
                            lv.reshape(b, s, MLSTM_HEADS, MLSTM_DV), li + gate_b[0], lf + gate_b[1]).astype(x.dtype)
    y_nsa = _nsa_attention(heads(nq, NSA_HEADS), kc, vc, ksl, vsl, kwn, vwn,
                           ng.reshape(b, s, NSA_HEADS, 3), cmp_pe, cmp_w1, cmp_w2)
    g_moba, g_ml, g_nsa = jnp.split(head_g, [MOBA_WIDTH, MOBA_WIDTH + MLSTM_WIDTH])
    y = jnp.concatenate([_head_norm(y_moba, g_moba),
                         _head_norm(y_ml, g_ml) * jax.nn.sigmoid(lo),
                         _head_norm(y_nsa, g_nsa)], axis=-1)
    x = x + _rms_norm(y @ w_out, norm_g[1])
    hm = _rms_norm(x, norm_g[2])
    y = jnp.square(jax.nn.relu(hm @ w1)) @ w2
    return x + _rms_norm(y, norm_g[3])


def setup_inputs(seed: int = 0) -> dict:
    key = jax.random.key(seed)
    ks = jax.random.split(key, 14)
    f32 = jnp.float32

    def normal(k, shape, scale):
        return jax.random.normal(k, shape, f32) * scale

    qk_width = 2 * MLSTM_HEADS * MLSTM_DQK
    i_bias = normal(ks[5], (DEPTH, MLSTM_HEADS), 0.1)
    f_bias = jnp.linspace(3.0, 6.0, MLSTM_HEADS, dtype=f32)[None, :] + normal(ks[6], (DEPTH, MLSTM_HEADS), 0.1)
    return {
        'x': normal(ks[0], (BATCH, SEQ, D_MODEL), 1.0),
        'norm_g': 1.0 + normal(ks[1], (DEPTH, 4, D_MODEL), 0.02),
        'w_in': normal(ks[2], (DEPTH, D_MODEL, D_IN), D_MODEL ** -0.5),
        'mlstm_conv_w': normal(ks[3], (DEPTH, MLSTM_CONV, qk_width), MLSTM_CONV ** -0.5),
        'mlstm_conv_b': normal(ks[4], (DEPTH, qk_width), 0.02),
        'mlstm_gate_b': jnp.stack([i_bias, f_bias], axis=1),
        'nsa_cmp_pe': normal(ks[7], (DEPTH, 2, NSA_CMP_LEN, NSA_DH), 0.02),
        'nsa_cmp_w1': normal(ks[8], (DEPTH, 2, NSA_CMP_LEN * NSA_DH, NSA_CMP_HIDDEN), (NSA_CMP_LEN * NSA_DH) ** -0.5),
        'nsa_cmp_w2': normal(ks[9], (DEPTH, 2, NSA_CMP_HIDDEN, NSA_DH), NSA_CMP_HIDDEN ** -0.5),
        'head_g': 1.0 + normal(ks[10], (DEPTH, MIX_WIDTH), 0.02),
        'w_out': normal(ks[11], (DEPTH, MIX_WIDTH, D_MODEL), MIX_WIDTH ** -0.5),
        'w_mlp1': normal(ks[12], (DEPTH, D_MODEL, D_FF), D_MODEL ** -0.5),
        'w_mlp2': normal(ks[13], (DEPTH, D_FF, D_MODEL), D_FF ** -0.5),
    }


def reference(x, norm_g, w_in, mlstm_conv_w, mlstm_conv_b, mlstm_gate_b, nsa_cmp_pe, nsa_cmp_w1,
              nsa_cmp_w2, head_g, w_out, w_mlp1, w_mlp2):
    for layer in range(DEPTH):
        x = _hybrid_layer(x, norm_g[layer], w_in[layer], mlstm_conv_w[layer], mlstm_conv_b[layer],
                          mlstm_gate_b[layer], nsa_cmp_pe[layer], nsa_cmp_w1[layer], nsa_cmp_w2[layer],
                          head_g[layer], w_out[layer], w_mlp1[layer], w_mlp2[layer])
    return x
```

```python
import functools

import numpy as np
import jax
import jax.numpy as jnp
from jax import lax
from jax.experimental import pallas as pl
from jax.experimental.pallas import tpu as pltpu

F32 = jnp.float32
BF16 = jnp.bfloat16

D_MODEL = 1024
MOBA_HEADS = 4
MOBA_WIDTH = 256
MOBA_BLOCK = 256
MOBA_TOPK = 3
MLSTM_HEADS = 4
MLSTM_WIDTH = 512
MLSTM_DV = 128
MLSTM_DQK = 64
MLSTM_CONV = 4
NSA_HEADS = 4
NSA_WIDTH = 256
NSA_DH = 64
NSA_CMP_LEN = 32
NSA_CMP_STRIDE = 16
NSA_CMP_HIDDEN = 256
NSA_SEL_BLOCK = 64
NSA_TOPN = 8
NSA_WINDOW = 512
D_FF = 4 * D_MODEL
NORM_EPS = 1e-6
NEG_INF = -1e30

LANES = 128
HEAD_DH = 64

PROJ_TM = 512
MLSTM_L = 128
NSA_TQ = 128
NSA_KT = 256
HALO = 16

Z_MOBA = 3 * MOBA_WIDTH
Z_LQK = 2 * MLSTM_HEADS * MLSTM_DQK
Z_LV = MLSTM_WIDTH
Z_LO = MLSTM_WIDTH
Z_NQ = NSA_WIDTH
Z_NKV = 6 * NSA_DH
Z_SMALL = LANES
Z_GROUPS = (Z_MOBA, Z_LQK, Z_LV, Z_LO, Z_NQ, Z_NKV, Z_SMALL)
Z_TOTAL = sum(Z_GROUPS)
VMEM_LIMIT = 56 * 1024 * 1024


def _rms(x, eps=NORM_EPS):
    return x * lax.rsqrt(jnp.mean(x * x, axis=-1, keepdims=True) + eps)


def _split3(x):
    hi = x.astype(BF16)
    r1 = x - hi.astype(F32)
    mid = r1.astype(BF16)
    lo = (r1 - mid.astype(F32)).astype(BF16)
    return hi, mid, lo


def _dot(a, b):
    return jnp.dot(a, b, preferred_element_type=F32)


def _dot_t(a, b):
    return lax.dot_general(a, b, (((1,), (1,)), ((), ())), preferred_element_type=F32)


def _in_proj_kernel(x_ref, g_ref, w_ref, *out_refs):
    xn = (_rms(x_ref[...]) * g_ref[...]).astype(BF16)
    off = 0
    for ref, width in zip(out_refs, Z_GROUPS):
        ref[...] = _dot(xn, w_ref[:, off:off + width]).astype(ref.dtype)
        off += width


def _in_proj(x2, g, w):
    n = x2.shape[0]
    dts = (BF16, BF16, BF16, BF16, BF16, BF16, F32)
    return pl.pallas_call(
        _in_proj_kernel,
        grid=(n // PROJ_TM,),
        in_specs=[pl.BlockSpec((PROJ_TM, D_MODEL), lambda i: (i, 0)),
                  pl.BlockSpec((1, D_MODEL), lambda i: (0, 0)),
                  pl.BlockSpec((D_MODEL, Z_TOTAL), lambda i: (0, 0))],
        out_specs=[pl.BlockSpec((PROJ_TM, wd), lambda i: (i, 0)) for wd in Z_GROUPS],
        out_shape=[jax.ShapeDtypeStruct((n, wd), dt) for wd, dt in zip(Z_GROUPS, dts)],
        compiler_params=pltpu.CompilerParams(dimension_semantics=("parallel",),
                                             vmem_limit_bytes=VMEM_LIMIT),
        name="in_proj",
    )(x2, g, w)


def _moba_kernel(q_ref, k_ref, v_ref, g_ref, o_ref):
    i = pl.program_id(1)
    blk = MOBA_BLOCK
    seq = k_ref.shape[0]
    nb = seq // blk
    scale = HEAD_DH ** -0.5
    lane = lax.broadcasted_iota(jnp.int32, (1, LANES), 1)
    col = lax.broadcasted_iota(jnp.int32, (blk, LANES), 1)
    valid = col < i
    srow = lax.broadcasted_iota(jnp.int32, (LANES, seq), 0)
    scol = lax.broadcasted_iota(jnp.int32, (LANES, seq), 1)
    avg = jnp.where((scol >= srow * blk) & (scol < (srow + 1) * blk), 1.0 / blk, 0.0).astype(BF16)
    qpos = lax.broadcasted_iota(jnp.int32, (blk, blk), 0)
    kpos = lax.broadcasted_iota(jnp.int32, (blk, blk), 1)
    causal = kpos <= qpos

    for pair in range(MOBA_HEADS // 2):
        lo = pair * LANES
        kmean = _dot(avg, k_ref[:, lo:lo + LANES])
        km_hi = kmean.astype(BF16)
        km_lo = (kmean - km_hi.astype(F32)).astype(BF16)
        q_pair = q_ref[:, lo:lo + LANES] * jnp.asarray(scale, BF16)
        y_pair = jnp.zeros((blk, LANES), F32)
        for sub in range(2):
            in_head = (lane >= sub * HEAD_DH) & (lane < (sub + 1) * HEAD_DH)
            qh = jnp.where(in_head, q_pair, jnp.zeros_like(q_pair))
            gate = _dot_t(qh, km_hi) + _dot_t(qh, km_lo)
            gate = jnp.where(valid, gate, -jnp.inf)
            rank = jnp.zeros((blk, LANES), jnp.int32)
            for n2 in range(nb):
                gn = gate[:, n2:n2 + 1]
                beats = (gn > gate) | ((gn == gate) & (n2 < col))
                rank = rank + beats.astype(jnp.int32)
            sel = jnp.where((rank < MOBA_TOPK) & valid, 1.0, 0.0)

            def body(j, carry, qh=qh, sel=sel, lo=lo):
                m, l, acc = carry
                r0 = pl.multiple_of(j * blk, blk)
                kj = k_ref[pl.ds(r0, blk), lo:lo + LANES]
                vj = v_ref[pl.ds(r0, blk), lo:lo + LANES]
                s = _dot_t(qh, kj)
                selj = jnp.max(jnp.where(col == j, sel, 0.0), axis=1, keepdims=True) > 0.5
                s = jnp.where(selj, s, NEG_INF)
                m_new = jnp.maximum(m, jnp.max(s, axis=1, keepdims=True))
                alpha = jnp.exp(m - m_new)
                p = jnp.exp(s - m_new)
                l = alpha * l + jnp.sum(p, axis=1, keepdims=True)
                acc = alpha * acc + _dot(p.astype(BF16), vj)
                return m_new, l, acc

            init = (jnp.full((blk, 1), NEG_INF, F32), jnp.zeros((blk, 1), F32),
                    jnp.zeros((blk, LANES), F32))
            m, l, acc = lax.fori_loop(0, i, body, init)
            r0 = pl.multiple_of(i * blk, blk)
            kj = k_ref[pl.ds(r0, blk), lo:lo + LANES]
            vj = v_ref[pl.ds(r0, blk), lo:lo + LANES]
            s = jnp.where(causal, _dot_t(qh, kj), NEG_INF)
            m_new = jnp.maximum(m, jnp.max(s, axis=1, keepdims=True))
            alpha = jnp.exp(m - m_new)
            p = jnp.exp(s - m_new)
            l = alpha * l + jnp.sum(p, axis=1, keepdims=True)
            acc = alpha * acc + _dot(p.astype(BF16), vj)
            o = acc / l
            ss = jnp.sum(jnp.where(in_head, o * o, 0.0), axis=1, keepdims=True) * (1.0 / HEAD_DH)
            y = o * lax.rsqrt(ss + NORM_EPS) * g_ref[:, lo:lo + LANES]
            y_pair = y_pair + jnp.where(in_head, y, 0.0)
        o_ref[:, lo:lo + LANES] = y_pair.astype(o_ref.dtype)


def _moba(zm, g_moba, batch, seq):
    zm3 = zm.reshape(batch, seq, Z_MOBA)
    w = MOBA_WIDTH
    return pl.pallas_call(
        _moba_kernel,
        grid=(batch, seq // MOBA_BLOCK),
        in_specs=[pl.BlockSpec((None, MOBA_BLOCK, w), lambda b, i: (b, i, 0)),
                  pl.BlockSpec((None, seq, w), lambda b, i: (b, 0, 1)),
                  pl.BlockSpec((None, seq, w), lambda b, i: (b, 0, 2)),
                  pl.BlockSpec((1, w), lambda b, i: (0, 0))],
        out_specs=pl.BlockSpec((None, MOBA_BLOCK, w), lambda b, i: (b, i, 0)),
        out_shape=jax.ShapeDtypeStruct((batch, seq, w), BF16),
        compiler_params=pltpu.CompilerParams(dimension_semantics=("parallel", "parallel"),
                                             vmem_limit_bytes=VMEM_LIMIT),
        name="moba",
    )(zm3, zm3, zm3, g_moba)


def _mlstm_kernel(qk_ref, v_ref, og_ref, zs_ref, cw_ref, cb_ref, gb_ref, g_ref, o_ref, xpad_ref):
    seq = qk_ref.shape[0]
    L = MLSTM_L
    nc = seq // L
    nh = MLSTM_HEADS
    wqk = Z_LQK
    half = wqk // 2
    lane = lax.broadcasted_iota(jnp.int32, (1, LANES), 1)

    xpad_ref[0:HALO, :] = jnp.zeros((HALO, wqk), F32)

    def fill(c, _):
        r0 = pl.multiple_of(c * L, L)
        xpad_ref[pl.ds(r0 + HALO, L), :] = qk_ref[pl.ds(r0, L), :].astype(F32)
        return 0

    lax.fori_loop(0, nc, fill, 0)

    trow = lax.broadcasted_iota(jnp.int32, (L, L), 0)
    tcol = lax.broadcasted_iota(jnp.int32, (L, L), 1)
    causal = tcol <= trow
    tri = jnp.where(causal, 1.0, 0.0).astype(BF16)

    def chunk(c, carry):
        r0 = pl.multiple_of(c * L, L)
        xa = xpad_ref[pl.ds(r0, L + HALO), :]
        conv = xa * cw_ref[MLSTM_CONV - 1:MLSTM_CONV, :] + cb_ref[...]
        for tap in range(1, MLSTM_CONV):
            conv = conv + pltpu.roll(xa, tap, 0) * cw_ref[MLSTM_CONV - 1 - tap:MLSTM_CONV - tap, :]
        conv = conv[HALO:, :]
        act = conv * jax.nn.sigmoid(conv)

        gz = zs_ref[pl.ds(r0, L), :] + gb_ref[...]
        logf = jnp.minimum(gz, 0.0) - jnp.log(1.0 + jnp.exp(-jnp.abs(gz)))
        gl = jnp.where((lane >= nh) & (lane < 2 * nh), logf, gz)
        g_hi, g_mid, g_lo = _split3(gl)
        cs = _dot(tri, g_hi) + _dot(tri, g_mid) + _dot(tri, g_lo)
        gl_t = gl.T
        cs_t = cs.T

        new_carry = []
        for pair in range(nh // 2):
            qp = act[:, pair * LANES:(pair + 1) * LANES]
            kp = act[:, half + pair * LANES:half + (pair + 1) * LANES] * (MLSTM_DQK ** -0.5)
            kp_b = kp.astype(BF16)
            kp_t = kp.T.astype(BF16)
            for sub in range(2):
                h = pair * 2 + sub
                c_st, n_st, m_st = carry[h]
                in_head = (lane >= sub * MLSTM_DQK) & (lane < (sub + 1) * MLSTM_DQK)
                qh = jnp.where(in_head, qp, 0.0)
                qh_b = qh.astype(BF16)
                vh = v_ref[pl.ds(r0, L), h * MLSTM_DV:(h + 1) * MLSTM_DV]
                ig_col = gl[:, h:h + 1]
                b_col = cs[:, nh + h:nh + h + 1]
                ig_row = gl_t[h:h + 1, :]
                b_row = cs_t[nh + h:nh + h + 1, :]
                b_end = b_row[:, L - 1:L]
                d_log = jnp.where(causal, b_col + (ig_row - b_row), -jnp.inf)
                m_inter = b_col + m_st
                m_t = jnp.maximum(m_inter, jnp.max(d_log, axis=1, keepdims=True))
                s_intra = _dot_t(qh_b, kp_b) * jnp.exp(d_log - m_t)
                w_inter = jnp.exp(m_inter - m_t)
                num = (_dot(s_intra.astype(BF16), vh)
                       + _dot(qh_b, c_st.astype(BF16)) * w_inter)
                den = (jnp.sum(s_intra, axis=1, keepdims=True)
                       + jnp.sum(qh * n_st, axis=1, keepdims=True) * w_inter)
                hid = num / jnp.maximum(jnp.abs(den), jnp.exp(-m_t))
                og = og_ref[pl.ds(r0, L), h * MLSTM_DV:(h + 1) * MLSTM_DV].astype(F32)
                y = _rms(hid) * g_ref[:, h * MLSTM_DV:(h + 1) * MLSTM_DV] * jax.nn.sigmoid(og)
                o_ref[pl.ds(r0, L), h * MLSTM_DV:(h + 1) * MLSTM_DV] = y.astype(o_ref.dtype)
                w_end_row = b_end + (ig_row - b_row)
                a_end = jnp.max(w_end_row, axis=1, keepdims=True)
                m_new = jnp.maximum(b_end + m_st, a_end)
                decay = jnp.exp(b_end + m_st - m_new)
                e_col = jnp.exp(b_end + (ig_col - b_col) - m_new)
                ve = (vh.astype(F32) * e_col).astype(BF16)
                c_new = decay * c_st + _dot(kp_t, ve)
                n_new = decay * n_st + jnp.sum(kp * e_col, axis=0, keepdims=True)
                new_carry.append((c_new, n_new, m_new))
        return tuple(new_carry)

    init = tuple((jnp.zeros((LANES, MLSTM_DV), F32), jnp.zeros((1, LANES), F32),
                  jnp.zeros((1, 1), F32)) for _ in range(nh))
    lax.fori_loop(0, nc, chunk, init)


def _mlstm(zqk, zv, zo, zs, conv_w, conv_b, gate_bias, g_ml, batch, seq):
    w = MLSTM_WIDTH
    r3 = lambda t: t.reshape(batch, seq, t.shape[-1])
    full = lambda shape: pl.BlockSpec(shape, lambda b: (0,) * len(shape))
    return pl.pallas_call(
        _mlstm_kernel,
        grid=(batch,),
        in_specs=[pl.BlockSpec((None, seq, Z_LQK), lambda b: (b, 0, 0)),
                  pl.BlockSpec((None, seq, w), lambda b: (b, 0, 0)),
                  pl.BlockSpec((None, seq, w), lambda b: (b, 0, 0)),
                  pl.BlockSpec((None, seq, Z_SMALL), lambda b: (b, 0, 0)),
                  full((MLSTM_CONV, Z_LQK)), full((1, Z_LQK)), full((1, Z_SMALL)), full((1, w))],
        out_specs=pl.BlockSpec((None, seq, w), lambda b: (b, 0, 0)),
        out_shape=jax.ShapeDtypeStruct((batch, seq, w), BF16),
        scratch_shapes=[pltpu.VMEM((seq + HALO, Z_LQK), F32)],
        compiler_params=pltpu.CompilerParams(dimension_semantics=("parallel",),
                                             vmem_limit_bytes=VMEM_LIMIT),
        name="mlstm",
    )(r3(zqk), r3(zv), r3(zo), r3(zs), conv_w, conv_b, gate_bias, g_ml)


def _gelu_tanh(x):
    return 0.5 * x * (1.0 + jnp.tanh(0.7978845608028654 * (x + 0.044715 * (x * x * x))))


def _nsa_cmp_kernel(tk_ref, tv_ref, pe_ref, w1_ref, w2_ref, o_ref):
    rows = tk_ref.shape[0]
    rid = lax.broadcasted_iota(jnp.int32, (rows, 1), 0)
    out = jnp.zeros((rows, LANES), F32)
    for j, t_ref in enumerate((tk_ref, tv_ref)):
        t = t_ref[...].astype(F32)
        xa = (t + pe_ref[j, 0:1, :]).astype(BF16)
        xb = (t + pe_ref[j, 1:2, :]).astype(BF16)
        first = _dot(xa, w1_ref[j, 0])
        second = _dot(xb, w1_ref[j, 1])
        hid = first + pltpu.roll(second, rows - 1, 0)
        out = out + _dot(_gelu_tanh(hid).astype(BF16), w2_ref[j])
    o_ref[...] = jnp.where(rid < rows - 1, out, 0.0).astype(o_ref.dtype)


def _nsa_compress(tk, tv, pe, w1, w2, batch, seq):
    rows = seq // NSA_CMP_STRIDE
    wide = NSA_CMP_STRIDE * NSA_DH
    full = lambda shape: pl.BlockSpec(shape, lambda b: (0,) * len(shape))
    return pl.pallas_call(
        _nsa_cmp_kernel,
        grid=(batch,),
        in_specs=[pl.BlockSpec((None, rows, wide), lambda b: (b, 0, 0)),
                  pl.BlockSpec((None, rows, wide), lambda b: (b, 0, 0)),
                  full((2, 2, wide)), full((2, 2, wide, NSA_CMP_HIDDEN)),
                  full((2, NSA_CMP_HIDDEN, LANES))],
        out_specs=pl.BlockSpec((None, rows, LANES), lambda b: (b, 0, 0)),
        out_shape=jax.ShapeDtypeStruct((batch, rows, LANES), BF16),
        compiler_params=pltpu.CompilerParams(dimension_semantics=("parallel",),
                                             vmem_limit_bytes=VMEM_LIMIT),
        name="nsa_compress",
    )(tk, tv, pe, w1, w2)


def _online_update(carry, s, kv):
    m, l, acc = carry
    m_new = jnp.maximum(m, jnp.max(s, axis=1, keepdims=True))
    alpha = jnp.exp(m - m_new)
    p = jnp.exp(s - m_new)
    l = alpha * l + jnp.sum(p, axis=1, keepdims=True)
    acc = alpha * acc + _dot(p.astype(BF16), kv)
    return m_new, l, acc


def _nsa_kernel(q_ref, kvc_ref, kv_ref, zs_ref, ov_ref, g_ref, o_ref):
    i = pl.program_id(1)
    tq = NSA_TQ
    kt = NSA_KT
    nh = NSA_HEADS
    rows = nh * tq
    n_sel_pad = LANES
    scale = HEAD_DH ** -0.5
    lane = lax.broadcasted_iota(jnp.int32, (1, LANES), 1)
    q0 = i * tq
    tpos = q0 + lax.broadcasted_iota(jnp.int32, (tq, 1), 0)
    tpos4 = jnp.concatenate([tpos] * nh, axis=0)

    q = q_ref[...] * jnp.asarray(scale, BF16)
    pr = lax.broadcasted_iota(jnp.int32, (NSA_WIDTH, LANES), 0)
    pc = lax.broadcasted_iota(jnp.int32, (NSA_WIDTH, LANES), 1)
    qs = jnp.concatenate(
        [_dot(q, jnp.where((pr == pc + h * HEAD_DH) & (pc < HEAD_DH), 1.0, 0.0).astype(BF16))
         for h in range(nh)], axis=0).astype(BF16)

    kvc = kvc_ref[...]
    n_cmp = kvc.shape[0] - 1
    s_c = _dot_t(qs, kvc)
    vis = ((lane * NSA_CMP_STRIDE + (NSA_CMP_LEN - 1)) <= tpos4) & (lane < n_cmp)
    s_m = jnp.where(vis, s_c, NEG_INF)
    e = jnp.exp(s_m - jnp.max(s_m, axis=1, keepdims=True))
    p_c = jnp.where(vis, e / jnp.sum(e, axis=1, keepdims=True), 0.0)
    o_cmp = _dot(p_c.astype(BF16), kvc)
    p_sum = p_c[0:tq]
    for h in range(1, nh):
        p_sum = p_sum + p_c[h * tq:(h + 1) * tq]

    ov = ov_ref[...]
    ps_hi, ps_mid, ps_lo = _split3(p_sum)
    p_slc = _dot(ps_hi, ov) + _dot(ps_mid, ov) + _dot(ps_lo, ov)
    n_sel = kv_ref.shape[0] // NSA_SEL_BLOCK
    cur = jnp.right_shift(tpos, 6)
    forced = (lane == 0) | (lane == cur) | (lane == cur - 1)
    score = jnp.where((lane > cur) | (lane >= n_sel), -jnp.inf, jnp.where(forced, jnp.inf, p_slc))
    rank = jnp.zeros((tq, n_sel_pad), jnp.int32)
    for j2 in range(n_sel):
        sj = score[:, j2:j2 + 1]
        beats = (sj > score) | ((sj == score) & (j2 < lane))
        rank = rank + beats.astype(jnp.int32)
    sel = jnp.where((rank < NSA_TOPN) & (score > -jnp.inf), 1.0, 0.0).astype(BF16)

    def init():
        return (jnp.full((rows, 1), NEG_INF, F32), jnp.zeros((rows, 1), F32),
                jnp.zeros((rows, LANES), F32))

    erow = lax.broadcasted_iota(jnp.int32, (n_sel_pad, kt), 0)
    ecol = lax.broadcasted_iota(jnp.int32, (n_sel_pad, kt), 1)
    klane = lax.broadcasted_iota(jnp.int32, (1, kt), 1)

    def sel_body(j, carry):
        k0 = pl.multiple_of(j * kt, kt)
        kv = kv_ref[pl.ds(k0, kt), LANES:2 * LANES]
        expand = jnp.where(jnp.right_shift(k0 + ecol, 6) == erow, 1.0, 0.0).astype(BF16)
        allowed = (_dot(sel, expand) > 0.5) & ((k0 + klane) <= tpos)
        allowed4 = jnp.concatenate([jnp.where(allowed, 1.0, 0.0)] * nh, axis=0) > 0.5
        s = jnp.where(allowed4, _dot_t(qs, kv), NEG_INF)
        return _online_update(carry, s, kv)

    n_tiles = (q0 + tq + kt - 1) // kt
    _, l_s, acc_s = lax.fori_loop(0, n_tiles, sel_body, init())
    o_sel = acc_s / l_s

    def win_body(j, carry):
        k0 = pl.multiple_of(j * kt, kt)
        kv = kv_ref[pl.ds(k0, kt), 2 * LANES:3 * LANES]
        rel = tpos4 - (k0 + klane)
        s = jnp.where((rel >= 0) & (rel < NSA_WINDOW), _dot_t(qs, kv), NEG_INF)
        return _online_update(carry, s, kv)

    first = jnp.maximum(q0 - NSA_WINDOW + 1, 0) // kt
    _, l_w, acc_w = lax.fori_loop(first, n_tiles, win_body, init())
    o_win = acc_w / l_w

    gates = jax.nn.sigmoid(zs_ref[...])
    base = 2 * MLSTM_HEADS

    def gate_col(br):
        return jnp.concatenate([gates[:, base + 3 * h + br:base + 3 * h + br + 1] for h in range(nh)],
                               axis=0)

    mix = gate_col(0) * o_cmp + gate_col(1) * o_sel + gate_col(2) * o_win
    is_v = lane >= HEAD_DH
    ss = jnp.sum(jnp.where(is_v, mix * mix, 0.0), axis=1, keepdims=True) * (1.0 / HEAD_DH)
    yn = mix * lax.rsqrt(ss + NORM_EPS)
    ur = lax.broadcasted_iota(jnp.int32, (LANES, NSA_WIDTH), 0)
    uc = lax.broadcasted_iota(jnp.int32, (LANES, NSA_WIDTH), 1)
    out = jnp.zeros((tq, NSA_WIDTH), F32)
    for h in range(nh):
        yh = (yn[h * tq:(h + 1) * tq] * g_ref[h]).astype(BF16)
        place = jnp.where((ur >= HEAD_DH) & (uc == ur - HEAD_DH + h * HEAD_DH), 1.0, 0.0).astype(BF16)
        out = out + _dot(yh, place)
    o_ref[...] = out.astype(o_ref.dtype)


def _nsa(znq, kvc, znkv, zs, overlap, g_nsa4, batch, seq):
    r3 = lambda t: t.reshape(batch, seq, t.shape[-1])
    n_rows = seq // NSA_CMP_STRIDE
    full = lambda shape: pl.BlockSpec(shape, lambda b, i: (0,) * len(shape))
    return pl.pallas_call(
        _nsa_kernel,
        grid=(batch, seq // NSA_TQ),
        in_specs=[pl.BlockSpec((None, NSA_TQ, NSA_WIDTH), lambda b, i: (b, i, 0)),
                  pl.BlockSpec((None, n_rows, LANES), lambda b, i: (b, 0, 0)),
                  pl.BlockSpec((None, seq, Z_NKV), lambda b, i: (b, 0, 0)),
                  pl.BlockSpec((None, NSA_TQ, Z_SMALL), lambda b, i: (b, i, 0)),
                  full((LANES, LANES)), full((NSA_HEADS, 1, LANES))],
        out_specs=pl.BlockSpec((None, NSA_TQ, NSA_WIDTH), lambda b, i: (b, i, 0)),
        out_shape=jax.ShapeDtypeStruct((batch, seq, NSA_WIDTH), BF16),
        compiler_params=pltpu.CompilerParams(dimension_semantics=("parallel", "parallel"),
                                             vmem_limit_bytes=VMEM_LIMIT),
        name="nsa_attn",
    )(r3(znq), kvc, r3(znkv), r3(zs), overlap, g_nsa4)


def _out_mlp_kernel(x_ref, ym_ref, yl_ref, yn_ref, ng_ref, wo_ref, w1_ref, w2_ref, o_ref):
    a = (_dot(ym_ref[...], wo_ref[0:MOBA_WIDTH, :])
         + _dot(yl_ref[...], wo_ref[MOBA_WIDTH:MOBA_WIDTH + MLSTM_WIDTH, :])
         + _dot(yn_ref[...], wo_ref[MOBA_WIDTH + MLSTM_WIDTH:, :]))
    x1 = x_ref[...] + _rms(a) * ng_ref[1:2, :]
    hm = (_rms(x1) * ng_ref[2:3, :]).astype(BF16)
    acc = jnp.zeros_like(x1)
    step = D_MODEL
    for c in range(D_FF // step):
        hcol = _dot(hm, w1_ref[:, c * step:(c + 1) * step])
        act = jnp.square(jnp.maximum(hcol, 0.0)).astype(BF16)
        acc = acc + _dot(act, w2_ref[c * step:(c + 1) * step, :])
    o_ref[...] = x1 + _rms(acc) * ng_ref[3:4, :]


def _out_mlp(x2, ym, yl, yn, norm_g, wo, w1, w2):
    n = x2.shape[0]
    tm = PROJ_TM
    const = lambda shape: pl.BlockSpec(shape, lambda i: (0,) * len(shape),
                                       pipeline_mode=pl.Buffered(1))
    return pl.pallas_call(
        _out_mlp_kernel,
        grid=(n // tm,),
        in_specs=[pl.BlockSpec((tm, D_MODEL), lambda i: (i, 0)),
                  pl.BlockSpec((tm, MOBA_WIDTH), lambda i: (i, 0)),
                  pl.BlockSpec((tm, MLSTM_WIDTH), lambda i: (i, 0)),
                  pl.BlockSpec((tm, NSA_WIDTH), lambda i: (i, 0)),
                  const((4, D_MODEL)), const((D_MODEL, D_MODEL)),
                  const((D_MODEL, D_FF)), const((D_FF, D_MODEL))],
        out_specs=pl.BlockSpec((tm, D_MODEL), lambda i: (i, 0)),
        out_shape=jax.ShapeDtypeStruct((n, D_MODEL), F32),
        compiler_params=pltpu.CompilerParams(dimension_semantics=("parallel",),
                                             vmem_limit_bytes=VMEM_LIMIT),
        name="out_mlp",
    )(x2, ym, yl, yn, norm_g, wo, w1, w2)


def _overlap_matrix(seq):
    n_cmp = (seq - NSA_CMP_LEN) // NSA_CMP_STRIDE + 1
    n_sel = seq // NSA_SEL_BLOCK
    lo = np.arange(n_cmp)[:, None] * NSA_CMP_STRIDE
    sl = np.arange(n_sel)[None, :] * NSA_SEL_BLOCK
    ov = np.maximum(np.minimum(lo + NSA_CMP_LEN, sl + NSA_SEL_BLOCK) - np.maximum(lo, sl), 0) / NSA_CMP_LEN
    full = np.zeros((LANES, LANES), np.float32)
    full[:n_cmp, :n_sel] = ov
    return jnp.asarray(full, BF16)


def _permute_w_in(w):
    d = w.shape[0]
    c_gate = Z_MOBA + Z_LQK + Z_LV + Z_LO
    c_nq = c_gate + 2 * MLSTM_HEADS
    c_ng = c_nq + Z_NQ + Z_NKV
    n_small = 2 * MLSTM_HEADS + 3 * NSA_HEADS
    return jnp.concatenate(
        [w[:, :c_gate], w[:, c_nq:c_ng], w[:, c_gate:c_nq], w[:, c_ng:],
         jnp.zeros((d, Z_SMALL - n_small), w.dtype)], axis=1).astype(BF16)


def _layer(x2, batch, seq, norm_g, w_in, conv_w, conv_b, gate_b, cmp_pe, cmp_w1, cmp_w2, head_g,
           w_out, w1, w2, overlap):
    zm, zqk, zv, zo, znq, znkv, zs = _in_proj(x2, norm_g[0:1], _permute_w_in(w_in))

    g_moba = head_g[None, :MOBA_WIDTH]
    g_ml = head_g[None, MOBA_WIDTH:MOBA_WIDTH + MLSTM_WIDTH]
    g_nsa = head_g[MOBA_WIDTH + MLSTM_WIDTH:].reshape(NSA_HEADS, 1, NSA_DH)
    g_nsa4 = jnp.concatenate([jnp.zeros_like(g_nsa), g_nsa], axis=-1)

    y_moba = _moba(zm, g_moba, batch, seq)

    gate_bias = jnp.zeros((1, Z_SMALL), F32).at[0, :2 * MLSTM_HEADS].set(gate_b.reshape(-1))
    y_ml = _mlstm(zqk, zv, zo, zs, conv_w, conv_b[None, :], gate_bias, g_ml, batch, seq)

    rows = seq // NSA_CMP_STRIDE
    wide = NSA_CMP_STRIDE * NSA_DH
    znkv3 = znkv.reshape(batch, seq, Z_NKV)
    tk = znkv3[:, :, 0:NSA_DH].reshape(batch, rows, wide)
    tv = znkv3[:, :, NSA_DH:2 * NSA_DH].reshape(batch, rows, wide)
    pe = cmp_pe.reshape(2, 2, wide)
    w1c = cmp_w1.reshape(2, 2, wide, NSA_CMP_HIDDEN).astype(BF16)
    zpad = jnp.zeros((NSA_CMP_HIDDEN, NSA_DH), cmp_w2.dtype)
    w2c = jnp.stack([jnp.concatenate([cmp_w2[0], zpad], axis=1),
                     jnp.concatenate([zpad, cmp_w2[1]], axis=1)]).astype(BF16)
    kvc = _nsa_compress(tk, tv, pe, w1c, w2c, batch, seq)
    y_nsa = _nsa(znq, kvc, znkv, zs, overlap, g_nsa4, batch, seq)

    n = batch * seq
    return _out_mlp(x2, y_moba.reshape(n, MOBA_WIDTH), y_ml.reshape(n, MLSTM_WIDTH),
                    y_nsa.reshape(n, NSA_WIDTH), norm_g, w_out.astype(BF16),
                    w1.astype(BF16), w2.astype(BF16))


def kernel(x, norm_g, w_in, mlstm_conv_w, mlstm_conv_b, mlstm_gate_b, nsa_cmp_pe, nsa_cmp_w1,
           nsa_cmp_w2, head_g, w_out, w_mlp1, w_mlp2):
    batch, seq, d = x.shape
    assert d == D_MODEL and w_in.shape[-1] == Z_TOTAL - (Z_SMALL - 2 * MLSTM_HEADS - 3 * NSA_HEADS)
    assert seq % MOBA_BLOCK == 0 and seq % NSA_KT == 0 and seq // NSA_CMP_STRIDE == LANES
    assert (batch * seq) % PROJ_TM == 0
    overlap = _overlap_matrix(seq)
    x2 = x.reshape(batch * seq, d)
    for layer in range(norm_g.shape[0]):
        x2 = _layer(x2, batch, seq, norm_g[layer], w_in[layer], mlstm_conv_w[layer],
                    mlstm_conv_b[layer], mlstm_gate_b[layer], nsa_cmp_pe[layer], nsa_cmp_w1[layer],
                    nsa_cmp_w2[layer], head_g[layer], w_out[layer], w_mlp1[layer], w_mlp2[layer],
                    overlap)
    return x2.reshape(batch, seq, d)
```
